```python
import jax, jax.numpy as jnp
from jax import lax
import numpy as np

D_MODEL = 1024
BATCH = 2
SEQ = 8192
DEPTH = 2

POOL_WINDOWS = (2, 4, 8, 16)
POOL_GROUPS = 4
POOL_WIDTH = D_MODEL // 2
POOL_GROUP_DIM = POOL_WIDTH // POOL_GROUPS
CONV_WIDTH = D_MODEL // 2
CONV_KERNEL = 31
N_HEADS = 8
HEAD_DIM = 64
ATTN_WIDTH = N_HEADS * HEAD_DIM
Q_BLOCK = 128
N_BRANCHES = 3
IN_SIZES = (POOL_WIDTH, POOL_WIDTH, 2 * CONV_WIDTH, CONV_WIDTH, 3 * ATTN_WIDTH, ATTN_WIDTH, N_BRANCHES * D_MODEL)
IN_WIDTH = sum(IN_SIZES)
RMS_EPS = 1e-6
LN_EPS = 1e-5

kernel_name = "hybrid_pool_conv_stickbreak_gated_block"


def rms_norm(x, g):
    xf = x.astype(jnp.float32)
    y = xf * lax.rsqrt(jnp.mean(xf * xf, axis=-1, keepdims=True) + RMS_EPS)
    return (y * g.astype(jnp.float32)).astype(x.dtype)


def layer_norm(x, g, b):
    xf = x.astype(jnp.float32)
    mu = jnp.mean(xf, axis=-1, keepdims=True)
    var = jnp.mean(jnp.square(xf - mu), axis=-1, keepdims=True)
    y = (xf - mu) * lax.rsqrt(var + LN_EPS)
    return (y * g.astype(jnp.float32) + b.astype(jnp.float32)).astype(x.dtype)


def multiscale_pool(u, pool_w, pool_b, pool_scale):
    B, T, C = u.shape
    uf = u.astype(jnp.float32)
    cs = jnp.cumsum(uf, axis=1)
    pos = jnp.arange(T)
    diffs = []
    for g, w in enumerate(POOL_WINDOWS):
        sl = slice(g * POOL_GROUP_DIM, (g + 1) * POOL_GROUP_DIM)
        c = cs[..., sl]
        lagged = jnp.pad(c, ((0, 0), (w, 0), (0, 0)))[:, :T]
        count = jnp.minimum(pos + 1, w).astype(jnp.float32)[None, :, None]
        diffs.append((c - lagged) / count - uf[..., sl])
    d = jnp.stack(diffs, axis=2).astype(u.dtype)
    y = jnp.einsum('btgc,gcd->btgd', d, pool_w) + pool_b
    return y.reshape(B, T, C) * pool_scale


def conformer_conv(c2, conv_w, conv_b, ln_g, ln_b):
    a, b = jnp.split(c2, 2, axis=-1)
    u = a * jax.nn.sigmoid(b)
    u = lax.conv_general_dilated(
        u, conv_w[:, None, :].astype(u.dtype), window_strides=(1,),
        padding=[(CONV_KERNEL - 1, 0)], dimension_numbers=('NWC', 'WIO', 'NWC'),
        feature_group_count=CONV_WIDTH) + conv_b
    u = layer_norm(u, ln_g, ln_b)
    return jax.nn.silu(u)


def stick_breaking_attention(q, k, v):
    B, H, T, Dh = q.shape
    nb = T // Q_BLOCK
    scale = 1.0 / np.sqrt(Dh).astype(np.float32)
    qb = q.reshape(B, H, nb, Q_BLOCK, Dh).transpose(2, 0, 1, 3, 4)
    kf = k.astype(jnp.float32)
    vf = v.astype(jnp.float32)
    key_pos = jnp.arange(T)

    def block(args):
        q_blk, i = args
        z = jnp.einsum('bhqd,bhkd->bhqk', q_blk.astype(jnp.float32), kf) * scale
        q_pos = i * Q_BLOCK + jnp.arange(Q_BLOCK)
        mask = key_pos[None, :] < q_pos[:, None]
        log_beta = jax.nn.log_sigmoid(z)
        log_1m = jnp.where(mask, jax.nn.log_sigmoid(-z), 0.0)
        between = lax.cumsum(log_1m, axis=3, reverse=True) - log_1m
        w = jnp.where(mask, jnp.exp(log_beta + between), 0.0)
        return jnp.einsum('bhqk,bhkd->bhqd', w, vf)

    out = lax.map(block, (qb, jnp.arange(nb)))
    return out.transpose(1, 2, 0, 3, 4).reshape(B, H, T, Dh).astype(v.dtype)


def setup_inputs(seed: int = 0) -> dict:
    key = jax.random.key(seed)
    ks = jax.random.split(key, 16)
    f32 = jnp.float32
    L, D = DEPTH, D_MODEL
    nrm = lambda k, shape, fan_in: jax.random.normal(k, shape, f32) * (fan_in ** -0.5)
    return {
        "x": jax.random.normal(ks[0], (BATCH, SEQ, D), f32),
        "norm_pre": 1.0 + 0.02 * jax.random.normal(ks[1], (L, D), f32),
        "w_in": nrm(ks[2], (L, D, IN_WIDTH), D),
        "pool_w": nrm(ks[3], (L, POOL_GROUPS, POOL_GROUP_DIM, POOL_GROUP_DIM), POOL_GROUP_DIM),
        "pool_b": 0.02 * jax.random.normal(ks[4], (L, POOL_GROUPS, POOL_GROUP_DIM), f32),
        "pool_scale": 1.0 + 0.02 * jax.random.normal(ks[5], (L, POOL_WIDTH), f32),
        "w_pool_out": nrm(ks[6], (L, POOL_WIDTH, D), POOL_WIDTH),
        "conv_w": nrm(ks[7], (L, CONV_KERNEL, CONV_WIDTH), CONV_KERNEL),
        "conv_b": 0.02 * jax.random.normal(ks[8], (L, CONV_WIDTH), f32),
        "conv_ln_g": 1.0 + 0.02 * jax.random.normal(ks[9], (L, CONV_WIDTH), f32),
        "conv_ln_b": 0.02 * jax.random.normal(ks[10], (L, CONV_WIDTH), f32),
        "w_conv_out": nrm(ks[11], (L, CONV_WIDTH, D), CONV_WIDTH),
        "w_attn_out": nrm(ks[12], (L, ATTN_WIDTH, D), ATTN_WIDTH),
        "w_o": nrm(ks[13], (L, D, D), D),
        "norm_post": 1.0 + 0.02 * jax.random.normal(ks[14], (L, D), f32),
    }


def reference(x, norm_pre, w_in, pool_w, pool_b, pool_scale, w_pool_out, conv_w, conv_b,
              conv_ln_g, conv_ln_b, w_conv_out, w_attn_out, w_o, norm_post):
    B, T, D = x.shape
    split_at = list(np.cumsum(IN_SIZES)[:-1])
    for l in range(DEPTH):
        h = rms_norm(x, norm_pre[l])
        proj = jnp.einsum('btd,de->bte', h, w_in[l])
        p, gp, c2, gc, qkv, ga, gm = jnp.split(proj, split_at, axis=-1)

        ya = multiscale_pool(p, pool_w[l], pool_b[l], pool_scale[l]) * jax.nn.silu(gp)
        ya = ya @ w_pool_out[l]

        yb = conformer_conv(c2, conv_w[l], conv_b[l], conv_ln_g[l], conv_ln_b[l]) * jax.nn.silu(gc)
        yb = yb @ w_conv_out[l]

        q, k, v = jnp.split(qkv.reshape(B, T, 3, N_HEADS, HEAD_DIM).transpose(2, 0, 3, 1, 4), 3, axis=0)
        o = stick_breaking_attention(q[0], k[0], v[0])
        o = o.transpose(0, 2, 1, 3).reshape(B, T, ATTN_WIDTH) * jax.nn.silu(ga)
        yc = o @ w_attn_out[l]

        g = jax.nn.sigmoid(gm.reshape(B, T, N_BRANCHES, D))
        m = g[:, :, 0] * ya + g[:, :, 1] * yb + g[:, :, 2] * yc
        out = m @ w_o[l]
        x = x + rms_norm(out, norm_post[l])
    return x
```

```python
import functools

import jax
import jax.numpy as jnp
import numpy as np
from jax import lax
from jax.experimental import pallas as pl
from jax.experimental.pallas import tpu as pltpu

D_MODEL = 1024
POOL_WINDOWS = (2, 4, 8, 16)
POOL_GROUP_DIM = 128
POOL_WIDTH = 512
CONV_WIDTH = 512
CONV_KERNEL = 31
N_HEADS = 8
HEAD_DIM = 64
ATTN_WIDTH = 512
RMS_EPS = 1e-6
LN_EPS = 1e-5

COL_P, COL_GP, COL_C2, COL_GC, COL_QKV, COL_GA, COL_GM, COL_END = 0, 512, 1024, 2048, 2560, 4096, 4608, 7680

LANES = 128
POOL_HALO = 16
CONV_HALO = 32
CONV_ROWS = 32
TOKEN_TILE = 256
ATTN_BLOCK = 128
ATTN_STATIC_BLOCKS = 3
EXP_ZERO_F32 = -104.0
VMEM_LIMIT = 48 * 1024 * 1024

_BF16 = jnp.bfloat16
_F32 = jnp.float32


def _dot(a, b):
    return jnp.dot(a, b, preferred_element_type=_F32)


def _rms_norm(x, g):
    ms = jnp.mean(x * x, axis=-1, keepdims=True)
    return x * lax.rsqrt(ms + RMS_EPS) * g


def _const_spec(shape):
    nd = len(shape)
    return pl.BlockSpec(shape, lambda *_: (0,) * nd, pipeline_mode=pl.Buffered(1))


def _qkv_kernel(x_ref, g_ref, w_ref, o_ref):
    h = _rms_norm(x_ref[0], g_ref[...]).astype(_BF16)
    o_ref[0] = _dot(h, w_ref[...]).astype(_BF16)


def _qkv_call(x, g, w_qkv):
    B, T, D = x.shape
    n = w_qkv.shape[1]
    return pl.pallas_call(
        _qkv_kernel,
        grid=(B, T // TOKEN_TILE),
        in_specs=[
            pl.BlockSpec((1, TOKEN_TILE, D), lambda b, t: (b, t, 0)),
            _const_spec((1, D)),
            _const_spec((D, n)),
        ],
        out_specs=pl.BlockSpec((1, TOKEN_TILE, n), lambda b, t: (b, t, 0)),
        out_shape=jax.ShapeDtypeStruct((B, T, n), _BF16),
        compiler_params=pltpu.CompilerParams(
            dimension_semantics=("arbitrary", "arbitrary"), vmem_limit_bytes=VMEM_LIMIT),
        name="qkv_proj",
    )(x, g, w_qkv)


def _attn_kernel(q_ref, k_ref, v_ref, sfx_ref, o_ref, carry_ref, acc_ref):
    qi = pl.program_id(1)
    blk = ATTN_BLOCK
    scale = 1.0 / np.sqrt(HEAD_DIM).astype(np.float32)
    lane = lax.broadcasted_iota(jnp.int32, (blk, LANES), 1)
    row = lax.broadcasted_iota(jnp.int32, (blk, blk), 0)
    col = lax.broadcasted_iota(jnp.int32, (blk, blk), 1)
    sfx = sfx_ref[...]

    def visit(h, kb, masked):
        pair = h // 2
        lanes = slice(pair * LANES, (pair + 1) * LANES)
        in_head = (lane // HEAD_DIM) == (h % 2)
        start = pl.multiple_of(jnp.maximum(kb, 0) * blk, blk)
        q = jnp.where(in_head, q_ref[0, :, lanes], jnp.zeros((), _BF16))
        k = k_ref[0, pl.ds(start, blk), lanes]
        v = jnp.where(in_head, v_ref[0, pl.ds(start, blk), lanes], jnp.zeros((), _BF16))
        z = lax.dot_general(q, k, (((1,), (1,)), ((), ())), preferred_element_type=_F32) * scale
        sp = jnp.maximum(z, 0.0) + jnp.log(1.0 + jnp.exp(-jnp.abs(z)))
        log_1m = -sp
        if masked:
            key_pos = kb * blk + col
            valid = jnp.logical_and(key_pos >= 0, key_pos < qi * blk + row)
            log_1m = jnp.where(valid, log_1m, 0.0)
        hi = log_1m.astype(_BF16)
        lo = (log_1m - hi.astype(_F32)).astype(_BF16)
        st = _dot(hi, sfx) + _dot(lo, sfx)
        between = st[:, :blk] + carry_ref[h]
        w = jnp.exp(z - sp + between)
        if masked:
            w = jnp.where(valid, w, 0.0)
        acc_ref[pair] += _dot(w.astype(_BF16), v)
        carry_ref[h] += st[:, blk:]

    carry_ref[...] = jnp.zeros_like(carry_ref)
    acc_ref[...] = jnp.zeros_like(acc_ref)
    for h in range(N_HEADS):
        for d in range(ATTN_STATIC_BLOCKS):
            visit(h, qi - d, masked=True)

    def worst_carry():
        worst = carry_ref[0]
        for h in range(1, N_HEADS):
            worst = jnp.maximum(worst, carry_ref[h])
        return jnp.max(worst)

    def more(state):
        kb, worst = state
        return jnp.logical_and(kb >= 0, worst > EXP_ZERO_F32)

    def step(state):
        kb, _ = state
        for h in range(N_HEADS):
            visit(h, kb, masked=False)
        return kb - 1, worst_carry()

    lax.while_loop(more, step, (qi - ATTN_STATIC_BLOCKS, worst_carry()))
    for pair in range(N_HEADS // 2):
        o_ref[0, :, pair * LANES:(pair + 1) * LANES] = acc_ref[pair]


def _attn_call(qkv, sfx):
    B, T, _ = qkv.shape
    blk = ATTN_BLOCK
    return pl.pallas_call(
        _attn_kernel,
        grid=(B, T // blk),
        in_specs=[
            pl.BlockSpec((1, blk, ATTN_WIDTH), lambda b, i: (b, i, 0)),
            pl.BlockSpec((1, T, ATTN_WIDTH), lambda b, i: (b, 0, 1)),
            pl.BlockSpec((1, T, ATTN_WIDTH), lambda b, i: (b, 0, 2)),
            _const_spec((blk, 2 * blk)),
        ],
        out_specs=pl.BlockSpec((1, blk, ATTN_WIDTH), lambda b, i: (b, i, 0)),
        out_shape=jax.ShapeDtypeStruct((B, T, ATTN_WIDTH), _F32),
        scratch_shapes=[
            pltpu.VMEM((N_HEADS, blk, blk), _F32),
            pltpu.VMEM((N_HEADS // 2, blk, LANES), _F32),
        ],
        compiler_params=pltpu.CompilerParams(
            dimension_semantics=("arbitrary", "arbitrary"), vmem_limit_bytes=VMEM_LIMIT),
        name="stickbreak_attn",
    )(qkv, qkv, qkv, sfx)


def _block_kernel(x_ref, o_ref, npre_ref, w_ref, poolw_ref, poolb_ref, pools_ref, wpo_ref,
                  convw_ref, convb_ref, lng_ref, lnb_ref, wco_ref, wao_ref, wo_ref, npost_ref,
                  out_ref, pbuf, ubuf, cbuf):
    tm = TOKEN_TILE
    ti = pl.program_id(1)

    @pl.when(ti == 0)
    def _():
        pbuf[0:POOL_HALO, :] = jnp.zeros((POOL_HALO, POOL_WIDTH), _F32)
        ubuf[0:CONV_HALO, :] = jnp.zeros((CONV_HALO, CONV_WIDTH), _F32)

    x = x_ref[0]
    h = _rms_norm(x, npre_ref[...]).astype(_BF16)

    def proj(lo, hi):
        return _dot(h, w_ref[:, lo:hi])

    p = proj(COL_P, COL_GP)
    pbuf[POOL_HALO:POOL_HALO + tm, :] = p
    t_glob = ti * tm + lax.broadcasted_iota(jnp.int32, (tm, POOL_GROUP_DIM), 0)
    ys = []
    for g, win in enumerate(POOL_WINDOWS):
        lanes = slice(g * POOL_GROUP_DIM, (g + 1) * POOL_GROUP_DIM)
        pg = p[:, lanes]
        s = pg
        for i in range(1, win):
            s = s + pbuf[pl.ds(POOL_HALO - i, tm), lanes]
        count = jnp.minimum(t_glob + 1, win).astype(_F32)
        dg = s / count - pg
        ys.append(_dot(dg.astype(_BF16), poolw_ref[g]))
    pbuf[0:POOL_HALO, :] = p[tm - POOL_HALO:, :]
    y = (jnp.concatenate(ys, axis=-1) + poolb_ref[...]) * pools_ref[...]
    ua = y * jax.nn.silu(proj(COL_GP, COL_C2))
    ya = _dot(ua.astype(_BF16), wpo_ref[...])

    c2 = proj(COL_C2, COL_GC)
    u = c2[:, :CONV_WIDTH] * jax.nn.sigmoid(c2[:, CONV_WIDTH:])
    ubuf[CONV_HALO:CONV_HALO + tm, :] = u

    for c in range(tm // CONV_ROWS):
        r0 = c * CONV_ROWS
        acc = jnp.broadcast_to(convb_ref[...], (CONV_ROWS, CONV_WIDTH))
        for k in range(CONV_KERNEL):
            off = CONV_HALO - (CONV_KERNEL - 1) + k
            acc = acc + convw_ref[k:k + 1, :] * ubuf[r0 + off:r0 + off + CONV_ROWS, :]
        cbuf[r0:r0 + CONV_ROWS, :] = acc
    ubuf[0:CONV_HALO, :] = u[tm - CONV_HALO:, :]
    cv = cbuf[...]
    mu = jnp.mean(cv, axis=-1, keepdims=True)
    var = jnp.mean(jnp.square(cv - mu), axis=-1, keepdims=True)
    ln = (cv - mu) * lax.rsqrt(var + LN_EPS) * lng_ref[...] + lnb_ref[...]
    ub = jax.nn.silu(ln) * jax.nn.silu(proj(COL_GC, COL_QKV))
    yb = _dot(ub.astype(_BF16), wco_ref[...])

    uc = o_ref[0] * jax.nn.silu(proj(COL_GA - 1536, COL_GM - 1536))
    yc = _dot(uc.astype(_BF16), wao_ref[...])

    gm = jax.nn.sigmoid(proj(COL_GM - 1536, COL_END - 1536))
    m = gm[:, :D_MODEL] * ya + gm[:, D_MODEL:2 * D_MODEL] * yb + gm[:, 2 * D_MODEL:] * yc
    out = _dot(m.astype(_BF16), wo_ref[...])
    out_ref[0] = x + _rms_norm(out, npost_ref[...])


def _block_call(x, o, npre, w_rest, poolw, poolb, pools, wpo, convw, convb, lng, lnb, wco, wao, wo, npost):
    B, T, D = x.shape
    tm = TOKEN_TILE
    consts = (npre, w_rest, poolw, poolb, pools, wpo, convw, convb, lng, lnb, wco, wao, wo, npost)
    return pl.pallas_call(
        _block_kernel,
        grid=(B, T // tm),
        in_specs=[
            pl.BlockSpec((1, tm, D), lambda b, t: (b, t, 0)),
            pl.BlockSpec((1, tm, ATTN_WIDTH), lambda b, t: (b, t, 0)),
        ] + [_const_spec(c.shape) for c in consts],
        out_specs=pl.BlockSpec((1, tm, D), lambda b, t: (b, t, 0)),
        out_shape=jax.ShapeDtypeStruct((B, T, D), _F32),
        scratch_shapes=[
            pltpu.VMEM((POOL_HALO + tm, POOL_WIDTH), _F32),
            pltpu.VMEM((CONV_HALO + tm, CONV_WIDTH), _F32),
            pltpu.VMEM((tm, CONV_WIDTH), _F32),
        ],
        compiler_params=pltpu.CompilerParams(
            dimension_semantics=("arbitrary", "arbitrary"), vmem_limit_bytes=VMEM_LIMIT),
        name="mixer_block",
    )(x, o, *consts)


def _suffix_matrix():
    j = np.arange(ATTN_BLOCK)[:, None]
    s = np.arange(ATTN_BLOCK)[None, :]
    strict = (j > s).astype(np.float32)
    return jnp.asarray(np.concatenate([strict, np.ones_like(strict)], axis=1), dtype=_BF16)


def kernel(x, norm_pre, w_in, pool_w, pool_b, pool_scale, w_pool_out, conv_w, conv_b, conv_ln_g, conv_ln_b,
           w_conv_out, w_attn_out, w_o, norm_post):
    depth = w_in.shape[0]
    sfx = _suffix_matrix()
    row = lambda a: a.reshape(1, -1)
    for l in range(depth):
        w_l = w_in[l].astype(_BF16)
        w_qkv = w_l[:, COL_QKV:COL_GA]
        w_rest = jnp.concatenate([w_l[:, :COL_QKV], w_l[:, COL_GA:]], axis=1)
        qkv = _qkv_call(x, row(norm_pre[l]), w_qkv)
        o = _attn_call(qkv, sfx)
        x = _block_call(
            x, o, row(norm_pre[l]), w_rest, pool_w[l].astype(_BF16), row(pool_b[l]), row(pool_scale[l]),
            w_pool_out[l].astype(_BF16), conv_w[l], row(conv_b[l]), row(conv_ln_g[l]), row(conv_ln_b[l]),
            w_conv_out[l].astype(_BF16), w_attn_out[l].astype(_BF16), w_o[l].astype(_BF16), row(norm_post[l]))
    return x
```

```python
import jax
import jax.numpy as jnp
import numpy as np
from jax import lax
from jax.experimental import pallas as pl
from jax.experimental.pallas import tpu as pltpu

D_MODEL = 1024
POOL_WINDOWS = (2, 4, 8, 16)
POOL_GROUP_DIM = 128
POOL_WIDTH = 512
CONV_WIDTH = 512
CONV_KERNEL = 31
N_HEADS = 8
HEAD_DIM = 64
ATTN_WIDTH = 512
RMS_EPS = 1e-6
LN_EPS = 1e-5

COL_P, COL_GP, COL_C2, COL_GC, COL_QKV, COL_GA, COL_GM, COL_END = 0, 512, 1024, 2048, 2560, 4096, 4608, 7680
QKV_COLS = COL_GA - COL_QKV

LANES = 128
LANE_TILES = CONV_WIDTH // LANES
POOL_HALO = 16
CONV_HALO = 32
CONV_ROWS = 32
TOKEN_TILE = 256
ATTN_SUB = 128
ATTN_TILE = 512
ATTN_WINDOW = 3
EXP_ZERO_F32 = 104.0
MASK_BIAS = -1e30
LOG2E = float(np.log2(np.e))
VMEM_LIMIT = 48 * 1024 * 1024

_BF16 = jnp.bfloat16
_F32 = jnp.float32


def _dot(a, b):
    return jnp.dot(a, b, preferred_element_type=_F32)


def _dot_nt(a, b):
    return lax.dot_general(a, b, (((1,), (1,)), ((), ())), preferred_element_type=_F32)


def _rms_norm(x, g):
    ms = jnp.mean(x * x, axis=-1, keepdims=True)
    return x * lax.rsqrt(ms + RMS_EPS) * g


def _sigmoid(x):
    return 0.5 * jnp.tanh(0.5 * x) + 0.5


def _silu(x):
    half = 0.5 * x
    return half + half * jnp.tanh(half)


def _const_spec(shape):
    nd = len(shape)
    return pl.BlockSpec(shape, lambda *_: (0,) * nd, pipeline_mode=pl.Buffered(1))


def _qkv_kernel(x_ref, g_ref, w_ref, o_ref):
    h = _rms_norm(x_ref[0], g_ref[...]).astype(_BF16)
    o_ref[0] = _dot(h, w_ref[...]).astype(_BF16)


def _qkv_call(x, g, w_qkv):
    B, T, D = x.shape
    n = w_qkv.shape[1]
    return pl.pallas_call(
        _qkv_kernel,
        grid=(B, T // TOKEN_TILE),
        in_specs=[
            pl.BlockSpec((1, TOKEN_TILE, D), lambda b, t: (b, t, 0)),
            _const_spec((1, D)),
            _const_spec((D, n)),
        ],
        out_specs=pl.BlockSpec((1, TOKEN_TILE, n), lambda b, t: (b, t, 0)),
        out_shape=jax.ShapeDtypeStruct((B, T, n), _BF16),
        compiler_params=pltpu.CompilerParams(
            dimension_semantics=("arbitrary", "arbitrary"), vmem_limit_bytes=VMEM_LIMIT),
        name="qkv_proj",
    )(x, g, w_qkv)


def _softplus(z):
    return jnp.maximum(z, 0.0) + jnp.log(1.0 + jnp.exp2(jnp.abs(z) * (-LOG2E)))


def _split_bf16(a):
    hi = a.astype(_BF16)
    lo = (a - hi.astype(_F32)).astype(_BF16)
    return hi, lo


def _attn_kernel(q_ref, k_ref, v_ref, ss_ref, o_ref, tot_ref):
    step = pl.program_id(1)
    sub = ATTN_SUB
    nsub = ATTN_TILE // sub
    win = ATTN_WINDOW * sub
    scale = 1.0 / float(np.sqrt(HEAD_DIM))
    zero = jnp.zeros((), _BF16)
    ss = ss_ref[...]

    def head_halves(a):
        low_half = lax.broadcasted_iota(jnp.int32, a.shape, 1) < HEAD_DIM
        return jnp.concatenate([jnp.where(low_half, a, zero), jnp.where(low_half, zero, a)], axis=0)

    def suffix_and_total(sp_blk):
        hi, lo = _split_bf16(sp_blk)
        st = _dot(jnp.concatenate([hi, lo], axis=1), ss)
        return st[:, :sub], st[:, sub:]

    def window(sb, _):
        g = step * nsub + sb
        rows = pl.ds(pl.multiple_of(sb * sub, sub), sub)
        kstart = pl.multiple_of(jnp.maximum(g - (ATTN_WINDOW - 1), 0) * sub, sub)
        key_pos = kstart + lax.broadcasted_iota(jnp.int32, (sub, win), 1)
        q_pos = g * sub + lax.broadcasted_iota(jnp.int32, (sub, win), 0)
        bias = jnp.where(key_pos < q_pos, 0.0, MASK_BIAS)
        for pair in range(N_HEADS // 2):
            lanes = slice(pair * LANES, (pair + 1) * LANES)
            k_win = k_ref[0, pl.ds(kstart, win), lanes]
            v_win = v_ref[0, pl.ds(kstart, win), lanes]
            z2 = _dot_nt(head_halves(q_ref[0, rows, lanes] * scale), k_win)
            ws = []
            for hh in range(2):
                z = z2[hh * sub:(hh + 1) * sub] + bias
                sp = _softplus(z)
                lb = z - sp
                later = None
                w_blocks = [None] * ATTN_WINDOW
                for j in reversed(range(ATTN_WINDOW)):
                    cols = slice(j * sub, (j + 1) * sub)
                    sfx, tot = suffix_and_total(sp[:, cols])
                    arg = lb[:, cols] - sfx
                    if later is not None:
                        arg = arg - later
                    w_blocks[j] = jnp.exp(arg)
                    later = tot if later is None else later + tot
                tot_ref[sb * N_HEADS + 2 * pair + hh] = later
                ws.append(jnp.concatenate(w_blocks, axis=1).astype(_BF16))
            o_ref[0, rows, lanes] = _dot(jnp.concatenate(ws, axis=1), head_halves(v_win))
        return 0

    lax.fori_loop(0, nsub, window, 0)

    def least_total(lo, n):
        least = tot_ref[lo]
        for i in range(1, n):
            least = jnp.minimum(least, tot_ref[lo + i])
        return jnp.min(least)

    def tail(sb, _):
        g = step * nsub + sb
        rows = pl.ds(pl.multiple_of(sb * sub, sub), sub)

        def more(state):
            kb, least = state
            return jnp.logical_and(kb >= 0, least < EXP_ZERO_F32)

        def visit(state):
            kb, _ = state
            keys = pl.ds(pl.multiple_of(kb * sub, sub), sub)
            for pair in range(N_HEADS // 2):
                lanes = slice(pair * LANES, (pair + 1) * LANES)
                z2 = _dot_nt(head_halves(q_ref[0, rows, lanes] * scale), k_ref[0, keys, lanes])
                ws = []
                for hh in range(2):
                    h = sb * N_HEADS + 2 * pair + hh
                    z = z2[hh * sub:(hh + 1) * sub]
                    sp = _softplus(z)
                    sfx, tot = suffix_and_total(sp)
                    ws.append(jnp.exp(z - sp - sfx - tot_ref[h]).astype(_BF16))
                    tot_ref[h] += tot
                o_ref[0, rows, lanes] += _dot(jnp.concatenate(ws, axis=1), head_halves(v_ref[0, keys, lanes]))
            return kb - 1, least_total(sb * N_HEADS, N_HEADS)

        lax.while_loop(more, visit, (g - ATTN_WINDOW, least_total(sb * N_HEADS, N_HEADS)))
        return 0

    @pl.when(least_total(0, nsub * N_HEADS) < EXP_ZERO_F32)
    def _():
        lax.fori_loop(0, nsub, tail, 0)


def _attn_call(qkv, ss):
    B, T, _ = qkv.shape
    return pl.pallas_call(
        _attn_kernel,
        grid=(B, T // ATTN_TILE),
        in_specs=[
            pl.BlockSpec((1, ATTN_TILE, ATTN_WIDTH), lambda b, i: (b, i, 0)),
            pl.BlockSpec((1, T, ATTN_WIDTH), lambda b, i: (b, 0, 1)),
            pl.BlockSpec((1, T, ATTN_WIDTH), lambda b, i: (b, 0, 2)),
            _const_spec(ss.shape),
        ],
        out_specs=pl.BlockSpec((1, ATTN_TILE, ATTN_WIDTH), lambda b, i: (b, i, 0)),
        out_shape=jax.ShapeDtypeStruct((B, T, ATTN_WIDTH), _F32),
        scratch_shapes=[
            pltpu.VMEM((ATTN_TILE // ATTN_SUB * N_HEADS, ATTN_SUB, ATTN_SUB), _F32),
        ],
        compiler_params=pltpu.CompilerParams(
            dimension_semantics=("arbitrary", "arbitrary"), vmem_limit_bytes=VMEM_LIMIT),
        name="stickbreak_attn",
    )(qkv, qkv, qkv, ss)


def _suffix_matrix():
    j = np.arange(ATTN_SUB)[:, None]
    s = np.arange(ATTN_SUB)[None, :]
    later = (j > s).astype(np.float32)
    half = np.concatenate([later, np.ones_like(later)], axis=1)
    return jnp.asarray(np.concatenate([half, half], axis=0), dtype=_BF16)


def _block_kernel(x_ref, o_ref, npre_ref, w_ref, poolw_ref, poolb_ref, pools_ref, wpo_ref,
                  convw_ref, convb_ref, lng_ref, lnb_ref, wco_ref, wao_ref, wo_ref, npost_ref,
                  out_ref, pbuf, ubuf, cbuf):
    tm = TOKEN_TILE
    ti = pl.program_id(1)

    @pl.when(ti == 0)
    def _():
        pbuf[:, 0:POOL_HALO, :] = jnp.zeros((LANE_TILES, POOL_HALO, LANES), _F32)
        ubuf[:, 0:CONV_HALO, :] = jnp.zeros((LANE_TILES, CONV_HALO, LANES), _F32)

    x = x_ref[0]
    h = _rms_norm(x, npre_ref[...]).astype(_BF16)

    def proj(lo, hi):
        return _dot(h, w_ref[:, lo:hi])

    p = proj(COL_P, COL_GP)
    t_glob = ti * tm + lax.broadcasted_iota(jnp.int32, (tm, POOL_GROUP_DIM), 0)
    ys = []
    for g, win in enumerate(POOL_WINDOWS):
        pg = p[:, g * LANES:(g + 1) * LANES]
        pbuf[g, POOL_HALO:POOL_HALO + tm, :] = pg
        s = pg
        for i in range(1, win):
            s = s + pbuf[g, POOL_HALO - i:POOL_HALO - i + tm, :]
        pbuf[g, 0:POOL_HALO, :] = pg[tm - POOL_HALO:, :]
        count = jnp.minimum(t_glob + 1, win).astype(_F32)
        dg = s / count - pg
        ys.append(_dot(dg.astype(_BF16), poolw_ref[g]))
    y = (jnp.concatenate(ys, axis=-1) + poolb_ref[...]) * pools_ref[...]
    ua = y * _silu(proj(COL_GP, COL_C2))
    ya = _dot(ua.astype(_BF16), wpo_ref[...])

    c2 = proj(COL_C2, COL_GC)
    u = c2[:, :CONV_WIDTH] * _sigmoid(c2[:, CONV_WIDTH:])
    for j in range(LANE_TILES):
        ubuf[j, CONV_HALO:CONV_HALO + tm, :] = u[:, j * LANES:(j + 1) * LANES]
    for ci in range(tm // CONV_ROWS):
        r0 = ci * CONV_ROWS
        for j in range(LANE_TILES):
            lanes = slice(j * LANES, (j + 1) * LANES)
            acc = jnp.broadcast_to(convb_ref[:, lanes], (CONV_ROWS, LANES))
            for k in range(CONV_KERNEL):
                off = r0 + CONV_HALO - (CONV_KERNEL - 1) + k
                acc = acc + convw_ref[k:k + 1, lanes] * ubuf[j, off:off + CONV_ROWS, :]
            cbuf[r0:r0 + CONV_ROWS, lanes] = acc
    for j in range(LANE_TILES):
        ubuf[j, 0:CONV_HALO, :] = u[tm - CONV_HALO:, j * LANES:(j + 1) * LANES]
    cv = cbuf[...]
    mu = jnp.mean(cv, axis=-1, keepdims=True)
    var = jnp.mean(jnp.square(cv - mu), axis=-1, keepdims=True)
    ln = (cv - mu) * lax.rsqrt(var + LN_EPS) * lng_ref[...] + lnb_ref[...]
    ub = _silu(ln) * _silu(proj(COL_GC, COL_QKV))
    yb = _dot(ub.astype(_BF16), wco_ref[...])

    uc = o_ref[0] * _silu(proj(COL_GA - QKV_COLS, COL_GM - QKV_COLS))
    yc = _dot(uc.astype(_BF16), wao_ref[...])

    gm = _sigmoid(proj(COL_GM - QKV_COLS, COL_END - QKV_COLS))
    m = gm[:, :D_MODEL] * ya + gm[:, D_MODEL:2 * D_MODEL] * yb + gm[:, 2 * D_MODEL:] * yc
    out = _dot(m.astype(_BF16), wo_ref[...])
    out_ref[0] = x + _rms_norm(out, npost_ref[...])


def _block_call(x, o, npre, w_rest, poolw, poolb, pools, wpo, convw, convb, lng, lnb, wco, wao, wo, npost):
    B, T, D = x.shape
    tm = TOKEN_TILE
    consts = (npre, w_rest, poolw, poolb, pools, wpo, convw, convb, lng, lnb, wco, wao, wo, npost)
    return pl.pallas_call(
        _block_kernel,
        grid=(B, T // tm),
        in_specs=[
            pl.BlockSpec((1, tm, D), lambda b, t: (b, t, 0)),
            pl.BlockSpec((1, tm, ATTN_WIDTH), lambda b, t: (b, t, 0)),
        ] + [_const_spec(c.shape) for c in consts],
        out_specs=pl.BlockSpec((1, tm, D), lambda b, t: (b, t, 0)),
        out_shape=jax.ShapeDtypeStruct((B, T, D), _F32),
        scratch_shapes=[
            pltpu.VMEM((LANE_TILES, POOL_HALO + tm, LANES), _F32),
            pltpu.VMEM((LANE_TILES, CONV_HALO + tm, LANES), _F32),
            pltpu.VMEM((tm, CONV_WIDTH), _F32),
        ],
        compiler_params=pltpu.CompilerParams(
            dimension_semantics=("arbitrary", "arbitrary"), vmem_limit_bytes=VMEM_LIMIT),
        name="mixer_block",
    )(x, o, *consts)


def kernel(x, norm_pre, w_in, pool_w, pool_b, pool_scale, w_pool_out, conv_w, conv_b, conv_ln_g, conv_ln_b,
           w_conv_out, w_attn_out, w_o, norm_post):
    depth = w_in.shape[0]
    ss = _suffix_matrix()
    row = lambda a: a.reshape(1, -1)
    for l in range(depth):
        w_l = w_in[l].astype(_BF16)
        w_qkv = w_l[:, COL_QKV:COL_GA]
        w_rest = jnp.concatenate([w_l[:, :COL_QKV], w_l[:, COL_GA:]], axis=1)
        qkv = _qkv_call(x, row(norm_pre[l]), w_qkv)
        o = _attn_call(qkv, ss)
        x = _block_call(
            x, o, row(norm_pre[l]), w_rest, pool_w[l].astype(_BF16), row(pool_b[l]), row(pool_scale[l]),
            w_pool_out[l].astype(_BF16), conv_w[l], row(conv_b[l]), row(conv_ln_g[l]), row(conv_ln_b[l]),
            w_conv_out[l].astype(_BF16), w_attn_out[l].astype(_BF16), w_o[l].astype(_BF16), row(norm_post[l]))
    return x
```

```python
import jax
import jax.numpy as jnp
import numpy as np
from jax import lax
from jax.experimental import pallas as pl
from jax.experimental.pallas import tpu as pltpu

D_MODEL = 1024
POOL_WINDOWS = (2, 4, 8, 16)
POOL_GROUP_DIM = 128
POOL_WIDTH = 512
CONV_WIDTH = 512
CONV_KERNEL = 31
N_HEADS = 8
HEAD_DIM = 64
ATTN_WIDTH = 512
RMS_EPS = 1e-6
LN_EPS = 1e-5

COL_P, COL_GP, COL_C2, COL_GC, COL_QKV, COL_GA, COL_GM, COL_END = 0, 512, 1024, 2048, 2560, 4096, 4608, 7680

LANES = 128
LANE_TILES = CONV_WIDTH // LANES
POOL_HALO = 16
CONV_HALO = 32
CONV_ROWS = 32
TOKEN_TILE = 512
QKV_TILE = 512
ATTN_SUB = 128
ATTN_TILE = 512
ATTN_WINDOW = 3
EXP_ZERO_F32 = 104.0
MASK_BIAS = -1e30
LOG2E = float(np.log2(np.e))
VMEM_LIMIT = 48 * 1024 * 1024

_BF16 = jnp.bfloat16
_F32 = jnp.float32


def _dot(a, b):
    return jnp.dot(a, b, preferred_element_type=_F32)


def _dot_nt(a, b):
    return lax.dot_general(a, b, (((1,), (1,)), ((), ())), preferred_element_type=_F32)


def _rms_norm(x, g):
    ms = jnp.mean(x * x, axis=-1, keepdims=True)
    return x * lax.rsqrt(ms + RMS_EPS) * g


def _silu_of_half(xh):
    return xh + xh * jnp.tanh(xh)


def _const_spec(shape):
    nd = len(shape)
    return pl.BlockSpec(shape, lambda *_: (0,) * nd, pipeline_mode=pl.Buffered(1))


def _w_in_spec(col, width):
    assert col % width == 0
    return pl.BlockSpec((D_MODEL, width), lambda *_: (0, col // width), pipeline_mode=pl.Buffered(1))


def _qkv_kernel(x_ref, g_ref, wq_ref, wk_ref, wv_ref, o_ref):
    h = _rms_norm(x_ref[0], g_ref[...]).astype(_BF16)
    for i, w_ref in enumerate((wq_ref, wk_ref, wv_ref)):
        o_ref[0, :, i * ATTN_WIDTH:(i + 1) * ATTN_WIDTH] = _dot(h, w_ref[...]).astype(_BF16)


def _qkv_call(x, g, w_l):
    B, T, D = x.shape
    tm = QKV_TILE
    return pl.pallas_call(
        _qkv_kernel,
        grid=(B, T // tm),
        in_specs=[
            pl.BlockSpec((1, tm, D), lambda b, t: (b, t, 0)),
            _const_spec((1, D)),
            _w_in_spec(COL_QKV, ATTN_WIDTH),
            _w_in_spec(COL_QKV + ATTN_WIDTH, ATTN_WIDTH),
            _w_in_spec(COL_QKV + 2 * ATTN_WIDTH, ATTN_WIDTH),
        ],
        out_specs=pl.BlockSpec((1, tm, 3 * ATTN_WIDTH), lambda b, t: (b, t, 0)),
        out_shape=jax.ShapeDtypeStruct((B, T, 3 * ATTN_WIDTH), _BF16),
        compiler_params=pltpu.CompilerParams(
            dimension_semantics=("arbitrary", "arbitrary"), vmem_limit_bytes=VMEM_LIMIT),
        name="qkv_proj",
    )(x, g, w_l, w_l, w_l)


def _softplus(z):
    return jnp.maximum(z, 0.0) + jnp.log(1.0 + jnp.exp2(jnp.abs(z) * (-LOG2E)))


def _split_bf16(a):
    hi = a.astype(_BF16)
    lo = (a - hi.astype(_F32)).astype(_BF16)
    return hi, lo


def _attn_kernel(q_ref, k_ref, v_ref, ss_ref, o_ref, tot_ref):
    step = pl.program_id(1)
    sub = ATTN_SUB
    nsub = ATTN_TILE // sub
    win = ATTN_WINDOW * sub
    scale = 1.0 / float(np.sqrt(HEAD_DIM))
    zero = jnp.zeros((), _BF16)
    ss = ss_ref[...]

    def head_halves(a):
        low_half = lax.broadcasted_iota(jnp.int32, a.shape, 1) < HEAD_DIM
        return jnp.concatenate([jnp.where(low_half, a, zero), jnp.where(low_half, zero, a)], axis=0)

    def suffix_and_total(sp_blk):
        hi, lo = _split_bf16(sp_blk)
        st = _dot(jnp.concatenate([hi, lo], axis=1), ss)
        return st[:, :sub], st[:, sub:]

    def window(sb, _):
        g = step * nsub + sb
        rows = pl.ds(pl.multiple_of(sb * sub, sub), sub)
        kstart = pl.multiple_of(jnp.maximum(g - (ATTN_WINDOW - 1), 0) * sub, sub)
        key_pos = kstart + lax.broadcasted_iota(jnp.int32, (sub, win), 1)
        q_pos = g * sub + lax.broadcasted_iota(jnp.int32, (sub, win), 0)
        bias = jnp.where(key_pos < q_pos, 0.0, MASK_BIAS)
        for pair in range(N_HEADS // 2):
            lanes = slice(pair * LANES, (pair + 1) * LANES)
            k_win = k_ref[0, pl.ds(kstart, win), lanes]
            v_win = v_ref[0, pl.ds(kstart, win), lanes]
            z2 = _dot_nt(head_halves(q_ref[0, rows, lanes] * scale), k_win)
            ws = []
            for hh in range(2):
                z = z2[hh * sub:(hh + 1) * sub] + bias
                sp = _softplus(z)
                later = None
                w_blocks = [None] * ATTN_WINDOW
                for j in reversed(range(ATTN_WINDOW)):
                    cols = slice(j * sub, (j + 1) * sub)
                    sfx, tot = suffix_and_total(sp[:, cols])
                    arg = z[:, cols] - sfx
                    if later is not None:
                        arg = arg - later
                    w_blocks[j] = jnp.exp(arg)
                    later = tot if later is None else later + tot
                tot_ref[sb * N_HEADS + 2 * pair + hh] = later
                ws.append(jnp.concatenate(w_blocks, axis=1).astype(_BF16))
            o_ref[0, rows, lanes] = _dot(jnp.concatenate(ws, axis=1), head_halves(v_win))
        return 0

    lax.fori_loop(0, nsub, window, 0)

    def least_total(lo, n):
        least = tot_ref[lo]
        for i in range(1, n):
            least = jnp.minimum(least, tot_ref[lo + i])
        return jnp.min(least)

    def tail(sb, _):
        g = step * nsub + sb
        rows = pl.ds(pl.multiple_of(sb * sub, sub), sub)

        def more(state):
            kb, least = state
            return jnp.logical_and(kb >= 0, least < EXP_ZERO_F32)

        def visit(state):
            kb, _ = state
            keys = pl.ds(pl.multiple_of(kb * sub, sub), sub)
            for pair in range(N_HEADS // 2):
                lanes = slice(pair * LANES, (pair + 1) * LANES)
                z2 = _dot_nt(head_halves(q_ref[0, rows, lanes] * scale), k_ref[0, keys, lanes])
                ws = []
                for hh in range(2):
                    h = sb * N_HEADS + 2 * pair + hh
                    z = z2[hh * sub:(hh + 1) * sub]
                    sfx, tot = suffix_and_total(_softplus(z))
                    ws.append(jnp.exp(z - sfx - tot_ref[h]).astype(_BF16))
                    tot_ref[h] += tot
                o_ref[0, rows, lanes] += _dot(jnp.concatenate(ws, axis=1), head_halves(v_ref[0, keys, lanes]))
            return kb - 1, least_total(sb * N_HEADS, N_HEADS)

        lax.while_loop(more, visit, (g - ATTN_WINDOW, least_total(sb * N_HEADS, N_HEADS)))
        return 0

    @pl.when(least_total(0, nsub * N_HEADS) < EXP_ZERO_F32)
    def _():
        lax.fori_loop(0, nsub, tail, 0)


def _attn_call(qkv, ss):
    B, T, _ = qkv.shape
    return pl.pallas_call(
        _attn_kernel,
        grid=(B, T // ATTN_TILE),
        in_specs=[
            pl.BlockSpec((1, ATTN_TILE, ATTN_WIDTH), lambda b, i: (b, i, 0)),
            pl.BlockSpec((1, T, ATTN_WIDTH), lambda b, i: (b, 0, 1)),
            pl.BlockSpec((1, T, ATTN_WIDTH), lambda b, i: (b, 0, 2)),
            _const_spec(ss.shape),
        ],
        out_specs=pl.BlockSpec((1, ATTN_TILE, ATTN_WIDTH), lambda b, i: (b, i, 0)),
        out_shape=jax.ShapeDtypeStruct((B, T, ATTN_WIDTH), _F32),
        scratch_shapes=[
            pltpu.VMEM((ATTN_TILE // ATTN_SUB * N_HEADS, ATTN_SUB, ATTN_SUB), _F32),
        ],
        compiler_params=pltpu.CompilerParams(
            dimension_semantics=("arbitrary", "arbitrary"), vmem_limit_bytes=VMEM_LIMIT),
        name="stickbreak_attn",
    )(qkv, qkv, qkv, ss)


def _suffix_matrix():
    j = np.arange(ATTN_SUB)[:, None]
    s = np.arange(ATTN_SUB)[None, :]
    from_s_on = (j >= s).astype(np.float32)
    half = np.concatenate([from_s_on, np.ones_like(from_s_on)], axis=1)
    return jnp.asarray(np.concatenate([half, half], axis=0), dtype=_BF16)


def _block_kernel(x_ref, o_ref, npre_ref, wp_ref, wgp_ref, wc2_ref, wgc_ref, wga_ref, wgm0_ref, wgm1_ref,
                  poolw_ref, poolb_ref, pools_ref, wpo_ref, convw_ref, convb_ref, lng_ref, lnb_ref,
                  wco_ref, wao_ref, wo_ref, npost_ref, out_ref, pbuf, ubuf, cbuf):
    tm = TOKEN_TILE
    ti = pl.program_id(1)

    @pl.when(ti == 0)
    def _():
        pbuf[:, 0:POOL_HALO, :] = jnp.zeros((LANE_TILES, POOL_HALO, LANES), _F32)
        ubuf[:, 0:CONV_HALO, :] = jnp.zeros((LANE_TILES, CONV_HALO, LANES), _F32)

    x = x_ref[0]
    h = _rms_norm(x, npre_ref[...]).astype(_BF16)

    p = _dot(h, wp_ref[...])
    t_glob = ti * tm + lax.broadcasted_iota(jnp.int32, (tm, POOL_GROUP_DIM), 0)
    ys = []
    for g, win in enumerate(POOL_WINDOWS):
        pg = p[:, g * LANES:(g + 1) * LANES]
        pbuf[g, POOL_HALO:POOL_HALO + tm, :] = pg
        s = pg
        for i in range(1, win):
            s = s + pbuf[g, POOL_HALO - i:POOL_HALO - i + tm, :]
        pbuf[g, 0:POOL_HALO, :] = pg[tm - POOL_HALO:, :]
        count = jnp.minimum(t_glob + 1, win).astype(_F32)
        dg = s / count - pg
        ys.append(_dot(dg.astype(_BF16), poolw_ref[g]))
    y = (jnp.concatenate(ys, axis=-1) + poolb_ref[...]) * pools_ref[...]
    ua = y * _silu_of_half(_dot(h, wgp_ref[...]))
    ya = _dot(ua.astype(_BF16), wpo_ref[...])

    c2 = _dot(h, wc2_ref[...])
    a_half = c2[:, :CONV_WIDTH]
    u = a_half + a_half * jnp.tanh(c2[:, CONV_WIDTH:])
    for j in range(LANE_TILES):
        ubuf[j, CONV_HALO:CONV_HALO + tm, :] = u[:, j * LANES:(j + 1) * LANES]
    for ci in range(tm // CONV_ROWS):
        r0 = ci * CONV_ROWS
        for j in range(LANE_TILES):
            lanes = slice(j * LANES, (j + 1) * LANES)
            acc = jnp.broadcast_to(convb_ref[:, lanes], (CONV_ROWS, LANES))
            for k in range(CONV_KERNEL):
                off = r0 + CONV_HALO - (CONV_KERNEL - 1) + k
                acc = acc + convw_ref[k:k + 1, lanes] * ubuf[j, off:off + CONV_ROWS, :]
            cbuf[r0:r0 + CONV_ROWS, lanes] = acc
    for j in range(LANE_TILES):
        ubuf[j, 0:CONV_HALO, :] = u[tm - CONV_HALO:, j * LANES:(j + 1) * LANES]
    cv = cbuf[...]
    mu = jnp.mean(cv, axis=-1, keepdims=True)
    var = jnp.mean(jnp.square(cv - mu), axis=-1, keepdims=True)
    ln_half = (cv - mu) * lax.rsqrt(var + LN_EPS) * lng_ref[...] + lnb_ref[...]
    ub = _silu_of_half(ln_half) * _silu_of_half(_dot(h, wgc_ref[...]))
    yb = _dot(ub.astype(_BF16), wco_ref[...])

    uc = o_ref[0] * _silu_of_half(_dot(h, wga_ref[...]))
    yc = _dot(uc.astype(_BF16), wao_ref[...])

    t = jnp.tanh(jnp.concatenate([_dot(h, wgm0_ref[...]), _dot(h, wgm1_ref[...])], axis=-1))
    m2 = (t[:, :D_MODEL] * ya + t[:, D_MODEL:2 * D_MODEL] * yb + t[:, 2 * D_MODEL:] * yc) + (ya + yb + yc)
    out = _dot(m2.astype(_BF16), wo_ref[...])
    out_ref[0] = x + _rms_norm(out, npost_ref[...])


def _block_call(x, o, npre, w_l, poolw, poolb, pools, wpo, convw, convb, lng, lnb, wco, wao, wo, npost):
    B, T, D = x.shape
    tm = TOKEN_TILE
    gm_half = (COL_END - COL_GM) // 2
    tail_consts = (poolw, poolb, pools, wpo, convw, convb, lng, lnb, wco, wao, wo, npost)
    return pl.pallas_call(
        _block_kernel,
        grid=(B, T // tm),
        in_specs=[
            pl.BlockSpec((1, tm, D), lambda b, t: (b, t, 0)),
            pl.BlockSpec((1, tm, ATTN_WIDTH), lambda b, t: (b, t, 0)),
            _const_spec(npre.shape),
            _w_in_spec(COL_P, POOL_WIDTH),
            _w_in_spec(COL_GP, POOL_WIDTH),
            _w_in_spec(COL_C2, 2 * CONV_WIDTH),
            _w_in_spec(COL_GC, CONV_WIDTH),
            _w_in_spec(COL_GA, ATTN_WIDTH),
            _w_in_spec(COL_GM, gm_half),
            _w_in_spec(COL_GM + gm_half, gm_half),
        ] + [_const_spec(c.shape) for c in tail_consts],
        out_specs=pl.BlockSpec((1, tm, D), lambda b, t: (b, t, 0)),
        out_shape=jax.ShapeDtypeStruct((B, T, D), _F32),
        scratch_shapes=[
            pltpu.VMEM((LANE_TILES, POOL_HALO + tm, LANES), _F32),
            pltpu.VMEM((LANE_TILES, CONV_HALO + tm, LANES), _F32),
            pltpu.VMEM((tm, CONV_WIDTH), _F32),
        ],
        compiler_params=pltpu.CompilerParams(
            dimension_semantics=("arbitrary", "arbitrary"), vmem_limit_bytes=VMEM_LIMIT),
        name="mixer_block",
    )(x, o, npre, w_l, w_l, w_l, w_l, w_l, w_l, w_l, *tail_consts)


def _half_gate_scale():
    scale = np.full((1, COL_END), 0.5, np.float32)
    scale[:, COL_P:COL_GP] = 1.0
    scale[:, COL_QKV:COL_GA] = 1.0
    return jnp.asarray(scale)


def kernel(x, norm_pre, w_in, pool_w, pool_b, pool_scale, w_pool_out, conv_w, conv_b, conv_ln_g, conv_ln_b,
           w_conv_out, w_attn_out, w_o, norm_post):
    depth = w_in.shape[0]
    ss = _suffix_matrix()
    gate_scale = _half_gate_scale()
    row = lambda a: a.reshape(1, -1)
    for l in range(depth):
        w_l = (w_in[l] * gate_scale).astype(_BF16)
        qkv = _qkv_call(x, row(norm_pre[l]), w_l)
        o = _attn_call(qkv, ss)
        x = _block_call(
            x, o, row(norm_pre[l]), w_l, pool_w[l].astype(_BF16), row(pool_b[l]), row(pool_scale[l]),
            w_pool_out[l].astype(_BF16), conv_w[l], row(conv_b[l]), row(0.5 * conv_ln_g[l]),
            row(0.5 * conv_ln_b[l]), w_conv_out[l].astype(_BF16), w_attn_out[l].astype(_BF16),
            (0.5 * w_o[l]).astype(_BF16), row(norm_post[l]))
    return x
```

```python
import jax
import jax.numpy as jnp
import numpy as np
from jax import lax
from jax.experimental import pallas as pl
from jax.experimental.pallas import tpu as pltpu

D_MODEL = 1024
POOL_WINDOWS = (2, 4, 8, 16)
POOL_GROUP_DIM = 128
POOL_WIDTH = 512
CONV_WIDTH = 512
CONV_KERNEL = 31
N_HEADS = 8
HEAD_DIM = 64
ATTN_WIDTH = 512
RMS_EPS = 1e-6
LN_EPS = 1e-5

COL_P, COL_GP, COL_C2, COL_GC, COL_QKV, COL_GA, COL_GM, COL_END = 0, 512, 1024, 2048, 2560, 4096, 4608, 7680

LANES = 128
LANE_TILES = CONV_WIDTH // LANES
POOL_HALO = 16
CONV_HALO = 32
CONV_ROWS = 32
TOKEN_TILE = 512
QKV_TILE = 1024
ATTN_SUB = 128
ATTN_TILE = 512
ATTN_WINDOW = 3
ATTN_FAR_ROWS = 64
EXP_ZERO_F32 = 104.0
MASK_BIAS = -1e30
LOG2E = float(np.log2(np.e))
VMEM_LIMIT = 48 * 1024 * 1024

_BF16 = jnp.bfloat16
_F32 = jnp.float32


def _dot(a, b):
    return jnp.dot(a, b, preferred_element_type=_F32)


def _dot_nt(a, b):
    return lax.dot_general(a, b, (((1,), (1,)), ((), ())), preferred_element_type=_F32)


def _rms_norm(x, g):
    ms = jnp.mean(x * x, axis=-1, keepdims=True)
    return x * lax.rsqrt(ms + RMS_EPS) * g


def _silu_of_half(xh):
    return xh + xh * jnp.tanh(xh)


def _const_spec(shape):
    nd = len(shape)
    return pl.BlockSpec(shape, lambda *_: (0,) * nd, pipeline_mode=pl.Buffered(1))


def _w_in_spec(col, width):
    assert col % width == 0
    return pl.BlockSpec((D_MODEL, width), lambda *_: (0, col // width), pipeline_mode=pl.Buffered(1))


def _qkv_kernel(x_ref, g_ref, wq_ref, wk_ref, wv_ref, o_ref):
    h = _rms_norm(x_ref[0], g_ref[...]).astype(_BF16)
    for i, w_ref in enumerate((wq_ref, wk_ref, wv_ref)):
        o_ref[0, :, i * ATTN_WIDTH:(i + 1) * ATTN_WIDTH] = _dot(h, w_ref[...]).astype(_BF16)


def _qkv_call(x, g, w_l):
    B, T, D = x.shape
    tm = QKV_TILE
    return pl.pallas_call(
        _qkv_kernel,
        grid=(B, T // tm),
        in_specs=[
            pl.BlockSpec((1, tm, D), lambda b, t: (b, t, 0)),
            _const_spec((1, D)),
            _w_in_spec(COL_QKV, ATTN_WIDTH),
            _w_in_spec(COL_QKV + ATTN_WIDTH, ATTN_WIDTH),
            _w_in_spec(COL_QKV + 2 * ATTN_WIDTH, ATTN_WIDTH),
        ],
        out_specs=pl.BlockSpec((1, tm, 3 * ATTN_WIDTH), lambda b, t: (b, t, 0)),
        out_shape=jax.ShapeDtypeStruct((B, T, 3 * ATTN_WIDTH), _BF16),
        compiler_params=pltpu.CompilerParams(
            dimension_semantics=("arbitrary", "arbitrary"), vmem_limit_bytes=VMEM_LIMIT),
        name="qkv_proj",
    )(x, g, w_l, w_l, w_l)


def _softplus(z):
    return jnp.maximum(z, 0.0) + jnp.log(1.0 + jnp.exp2(jnp.abs(z) * (-LOG2E)))


def _split_bf16(a):
    hi = a.astype(_BF16)
    lo = (a - hi.astype(_F32)).astype(_BF16)
    return hi, lo


def _attn_kernel(q_ref, k_ref, v_ref, ss_ref, o_ref, tot_ref):
    step = pl.program_id(1)
    sub = ATTN_SUB
    nsub = ATTN_TILE // sub
    far = ATTN_FAR_ROWS
    scale = 1.0 / float(np.sqrt(HEAD_DIM))
    zero = jnp.zeros((), _BF16)
    ss = ss_ref[...]

    def head_halves(a):
        low_half = lax.broadcasted_iota(jnp.int32, a.shape, 1) < HEAD_DIM
        return jnp.concatenate([jnp.where(low_half, a, zero), jnp.where(low_half, zero, a)], axis=0)

    def suffix_and_total(sp_blk):
        hi, lo = _split_bf16(sp_blk)
        st = _dot(jnp.concatenate([hi, lo], axis=1), ss)
        return st[:, :sub], st[:, sub:]

    def window(sb, _):
        g = step * nsub + sb
        rows = pl.ds(pl.multiple_of(sb * sub, sub), sub)
        far_rows = pl.ds(pl.multiple_of(sb * sub, sub), far)
        starts = [pl.multiple_of(jnp.maximum(g - 2 + j, 0) * sub, sub) for j in range(ATTN_WINDOW)]

        def bias_for(n_rows, first_block, n_blocks):
            key_pos = (g - 2 + first_block) * sub + lax.broadcasted_iota(jnp.int32, (n_rows, n_blocks * sub), 1)
            q_pos = g * sub + lax.broadcasted_iota(jnp.int32, (n_rows, n_blocks * sub), 0)
            return jnp.where(jnp.logical_and(key_pos >= 0, key_pos < q_pos), 0.0, MASK_BIAS)

        bias_far = bias_for(far, 0, 1)
        bias_near = bias_for(sub, 1, 2)
        for pair in range(N_HEADS // 2):
            lanes = slice(pair * LANES, (pair + 1) * LANES)
            k_blk = [k_ref[0, pl.ds(st, sub), lanes] for st in starts]
            v_win = jnp.concatenate([v_ref[0, pl.ds(st, sub), lanes] for st in starts], axis=0)
            z_near2 = _dot_nt(head_halves(q_ref[0, rows, lanes] * scale), jnp.concatenate(k_blk[1:], axis=0))
            z_far2 = _dot_nt(head_halves(q_ref[0, far_rows, lanes] * scale), k_blk[0])
            ws = []
            for hh in range(2):
                z = z_near2[hh * sub:(hh + 1) * sub] + bias_near
                sp = _softplus(z)
                sfx, later = suffix_and_total(sp[:, sub:])
                w_diag = jnp.exp(z[:, sub:] - sfx)
                sfx, tot = suffix_and_total(sp[:, :sub])
                w_prev = jnp.exp(z[:, :sub] - sfx - later)
                later = later + tot
                zf = z_far2[hh * far:(hh + 1) * far] + bias_far
                sfx, tot = suffix_and_total(_softplus(zf))
                w_far = jnp.exp(zf - sfx - later[:far])
                tot_ref[sb * N_HEADS + 2 * pair + hh] = jnp.concatenate([later[:far] + tot, later[far:]], axis=0)
                w_far = jnp.concatenate([w_far, jnp.zeros((sub - far, sub), _F32)], axis=0)
                ws.append(jnp.concatenate([w_far, w_prev, w_diag], axis=1).astype(_BF16))
            o_ref[0, rows, lanes] = _dot(jnp.concatenate(ws, axis=1), head_halves(v_win))
        return 0

    lax.fori_loop(0, nsub, window, 0, unroll=2)

    def least_total(lo, n):
        least = tot_ref[lo]
        for i in range(1, n):
            least = jnp.minimum(least, tot_ref[lo + i])
        return jnp.min(least)

    def tail(sb, _):
        g = step * nsub + sb
        rows = pl.ds(pl.multiple_of(sb * sub, sub), sub)

        def more(state):
            kb, least = state
            return jnp.logical_and(kb >= 0, least < EXP_ZERO_F32)

        def visit(state):
            kb, _ = state
            keys = pl.ds(pl.multiple_of(kb * sub, sub), sub)
            done = jnp.logical_and(kb == g - 2, lax.broadcasted_iota(jnp.int32, (sub, sub), 0) < far)
            skip_bias = jnp.where(done, MASK_BIAS, 0.0)
            for pair in range(N_HEADS // 2):
                lanes = slice(pair * LANES, (pair + 1) * LANES)
                z2 = _dot_nt(head_halves(q_ref[0, rows, lanes] * scale), k_ref[0, keys, lanes])
                ws = []
                for hh in range(2):
                    h = sb * N_HEADS + 2 * pair + hh
                    z = z2[hh * sub:(hh + 1) * sub] + skip_bias
                    sfx, tot = suffix_and_total(_softplus(z))
                    ws.append(jnp.exp(z - sfx - tot_ref[h]).astype(_BF16))
                    tot_ref[h] += tot
                o_ref[0, rows, lanes] += _dot(jnp.concatenate(ws, axis=1), head_halves(v_ref[0, keys, lanes]))
            return kb - 1, least_total(sb * N_HEADS, N_HEADS)

        lax.while_loop(more, visit, (g - 2, least_total(sb * N_HEADS, N_HEADS)))
        return 0

    @pl.when(least_total(0, nsub * N_HEADS) < EXP_ZERO_F32)
    def _():
        lax.fori_loop(0, nsub, tail, 0)


def _attn_call(qkv, ss):
    B, T, _ = qkv.shape
    return pl.pallas_call(
        _attn_kernel,
        grid=(B, T // ATTN_TILE),
        in_specs=[
            pl.BlockSpec((1, ATTN_TILE, ATTN_WIDTH), lambda b, i: (b, i, 0)),
            pl.BlockSpec((1, T, ATTN_WIDTH), lambda b, i: (b, 0, 1)),
            pl.BlockSpec((1, T, ATTN_WIDTH), lambda b, i: (b, 0, 2)),
            _const_spec(ss.shape),
        ],
        out_specs=pl.BlockSpec((1, ATTN_TILE, ATTN_WIDTH), lambda b, i: (b, i, 0)),
        out_shape=jax.ShapeDtypeStruct((B, T, ATTN_WIDTH), _F32),
        scratch_shapes=[
            pltpu.VMEM((ATTN_TILE // ATTN_SUB * N_HEADS, ATTN_SUB, ATTN_SUB), _F32),
        ],
        compiler_params=pltpu.CompilerParams(
            dimension_semantics=("arbitrary", "arbitrary"), vmem_limit_bytes=VMEM_LIMIT),
        name="stickbreak_attn",
    )(qkv, qkv, qkv, ss)


def _suffix_matrix():
    j = np.arange(ATTN_SUB)[:, None]
    s = np.arange(ATTN_SUB)[None, :]
    from_s_on = (j >= s).astype(np.float32)
    half = np.concatenate([from_s_on, np.ones_like(from_s_on)], axis=1)
    return jnp.asarray(np.concatenate([half, half], axis=0), dtype=_BF16)


def _block_kernel(x_ref, o_ref, npre_ref, wp_ref, wgp_ref, wc2_ref, wgc_ref, wga_ref, wgm0_ref, wgm1_ref,
                  poolw_ref, poolb_ref, pools_ref, wpo_ref, convw_ref, convb_ref, lng_ref, lnb_ref,
                  wco_ref, wao_ref, wo_ref, npost_ref, out_ref, pbuf, ubuf, cbuf):
    tm = TOKEN_TILE
    ti = pl.program_id(1)

    @pl.when(ti == 0)
    def _():
        pbuf[:, 0:POOL_HALO, :] = jnp.zeros((LANE_TILES, POOL_HALO, LANES), _F32)
        ubuf[:, 0:CONV_HALO, :] = jnp.zeros((LANE_TILES, CONV_HALO, LANES), _F32)

    x = x_ref[0]
    h = _rms_norm(x, npre_ref[...]).astype(_BF16)

    p = _dot(h, wp_ref[...])
    t_glob = ti * tm + lax.broadcasted_iota(jnp.int32, (tm, POOL_GROUP_DIM), 0)
    ys = []
    for g, win in enumerate(POOL_WINDOWS):
        pg = p[:, g * LANES:(g + 1) * LANES]
        pbuf[g, POOL_HALO:POOL_HALO + tm, :] = pg
        s = pg
        for i in range(1, win):
            s = s + pbuf[g, POOL_HALO - i:POOL_HALO - i + tm, :]
        pbuf[g, 0:POOL_HALO, :] = pg[tm - POOL_HALO:, :]
        count = jnp.minimum(t_glob + 1, win).astype(_F32)
        dg = s / count - pg
        ys.append(_dot(dg.astype(_BF16), poolw_ref[g]))
    y = (jnp.concatenate(ys, axis=-1) + poolb_ref[...]) * pools_ref[...]
    ua = y * _silu_of_half(_dot(h, wgp_ref[...]))
    ya = _dot(ua.astype(_BF16), wpo_ref[...])

    c2 = _dot(h, wc2_ref[...])
    a_half = c2[:, :CONV_WIDTH]
    u = a_half + a_half * jnp.tanh(c2[:, CONV_WIDTH:])
    for j in range(LANE_TILES):
        ubuf[j, CONV_HALO:CONV_HALO + tm, :] = u[:, j * LANES:(j + 1) * LANES]
    for ci in range(tm // CONV_ROWS):
        r0 = ci * CONV_ROWS
        for j in range(LANE_TILES):
            lanes = slice(j * LANES, (j + 1) * LANES)
            acc = jnp.broadcast_to(convb_ref[:, lanes], (CONV_ROWS, LANES))
            for k in range(CONV_KERNEL):
                off = r0 + CONV_HALO - (CONV_KERNEL - 1) + k
                acc = acc + convw_ref[k:k + 1, lanes] * ubuf[j, off:off + CONV_ROWS, :]
            cbuf[r0:r0 + CONV_ROWS, lanes] = acc
    for j in range(LANE_TILES):
        ubuf[j, 0:CONV_HALO, :] = u[tm - CONV_HALO:, j * LANES:(j + 1) * LANES]
    cv = cbuf[...]
    mu = jnp.mean(cv, axis=-1, keepdims=True)
    var = jnp.mean(jnp.square(cv - mu), axis=-1, keepdims=True)
    ln_half = (cv - mu) * lax.rsqrt(var + LN_EPS) * lng_ref[...] + lnb_ref[...]
    ub = _silu_of_half(ln_half) * _silu_of_half(_dot(h, wgc_ref[...]))
    yb = _dot(ub.astype(_BF16), wco_ref[...])

    uc = o_ref[0] * _silu_of_half(_dot(h, wga_ref[...]))
    yc = _dot(uc.astype(_BF16), wao_ref[...])

    t = jnp.tanh(jnp.concatenate([_dot(h, wgm0_ref[...]), _dot(h, wgm1_ref[...])], axis=-1))
    m2 = (t[:, :D_MODEL] * ya + t[:, D_MODEL:2 * D_MODEL] * yb + t[:, 2 * D_MODEL:] * yc) + (ya + yb + yc)
    out = _dot(m2.astype(_BF16), wo_ref[...])
    out_ref[0] = x + _rms_norm(out, npost_ref[...])


def _block_call(x, o, npre, w_l, poolw, poolb, pools, wpo, convw, convb, lng, lnb, wco, wao, wo, npost):
    B, T, D = x.shape
    tm = TOKEN_TILE
    gm_half = (COL_END - COL_GM) // 2
    tail_consts = (poolw, poolb, pools, wpo, convw, convb, lng, lnb, wco, wao, wo, npost)
    return pl.pallas_call(
        _block_kernel,
        grid=(B, T // tm),
        in_specs=[
            pl.BlockSpec((1, tm, D), lambda b, t: (b, t, 0)),
            pl.BlockSpec((1, tm, ATTN_WIDTH), lambda b, t: (b, t, 0)),
            _const_spec(npre.shape),
            _w_in_spec(COL_P, POOL_WIDTH),
            _w_in_spec(COL_GP, POOL_WIDTH),
            _w_in_spec(COL_C2, 2 * CONV_WIDTH),
            _w_in_spec(COL_GC, CONV_WIDTH),
            _w_in_spec(COL_GA, ATTN_WIDTH),
            _w_in_spec(COL_GM, gm_half),
            _w_in_spec(COL_GM + gm_half, gm_half),
        ] + [_const_spec(c.shape) for c in tail_consts],
        out_specs=pl.BlockSpec((1, tm, D), lambda b, t: (b, t, 0)),
        out_shape=jax.ShapeDtypeStruct((B, T, D), _F32),
        scratch_shapes=[
            pltpu.VMEM((LANE_TILES, POOL_HALO + tm, LANES), _F32),
            pltpu.VMEM((LANE_TILES, CONV_HALO + tm, LANES), _F32),
            pltpu.VMEM((tm, CONV_WIDTH), _F32),
        ],
        compiler_params=pltpu.CompilerParams(
            dimension_semantics=("arbitrary", "arbitrary"), vmem_limit_bytes=VMEM_LIMIT),
        name="mixer_block",
    )(x, o, npre, w_l, w_l, w_l, w_l, w_l, w_l, w_l, *tail_consts)


def _half_gate_scale():
    scale = np.full((1, COL_END), 0.5, np.float32)
    scale[:, COL_P:COL_GP] = 1.0
    scale[:, COL_QKV:COL_GA] = 1.0
    return jnp.asarray(scale)


def kernel(x, norm_pre, w_in, pool_w, pool_b, pool_scale, w_pool_out, conv_w, conv_b, conv_ln_g, conv_ln_b,
           w_conv_out, w_attn_out, w_o, norm_post):
    depth = w_in.shape[0]
    ss = _suffix_matrix()
    gate_scale = _half_gate_scale()
    row = lambda a: a.reshape(1, -1)
    for l in range(depth):
        w_l = (w_in[l] * gate_scale).astype(_BF16)
        qkv = _qkv_call(x, row(norm_pre[l]), w_l)
        o = _attn_call(qkv, ss)
        x = _block_call(
            x, o, row(norm_pre[l]), w_l, pool_w[l].astype(_BF16), row(pool_b[l]), row(pool_scale[l]),
            w_pool_out[l].astype(_BF16), conv_w[l], row(conv_b[l]), row(0.5 * conv_ln_g[l]),
            row(0.5 * conv_ln_b[l]), w_conv_out[l].astype(_BF16), w_attn_out[l].astype(_BF16),
            (0.5 * w_o[l]).astype(_BF16), row(norm_post[l]))
    return x
```

```python
import jax
import jax.numpy as jnp
import numpy as np
from jax import lax
from jax.experimental import pallas as pl
from jax.experimental.pallas import tpu as pltpu

D_MODEL = 1024
POOL_WINDOWS = (2, 4, 8, 16)
POOL_GROUP_DIM = 128
POOL_WIDTH = 512
CONV_WIDTH = 512
CONV_KERNEL = 31
N_HEADS = 8
HEAD_DIM = 64
ATTN_WIDTH = 512
RMS_EPS = 1e-6
LN_EPS = 1e-5

COL_P, COL_GP, COL_C2, COL_GC, COL_QKV, COL_GA, COL_GM, COL_END = 0, 512, 1024, 2048, 2560, 4096, 4608, 7680

LANES = 128
LANE_TILES = CONV_WIDTH // LANES
POOL_HALO = 16
CONV_HALO = 32
CONV_ROWS = 32
TOKEN_TILE = 512
QKV_TILE = 1024
ATTN_SUB = 128
ATTN_TILE = 512
ATTN_WINDOW = 3
ATTN_FAR_ROWS = 64
EXP_ZERO_F32 = 104.0
MASK_BIAS = -1e30
LOG2E = float(np.log2(np.e))
VMEM_LIMIT = 48 * 1024 * 1024

_BF16 = jnp.bfloat16
_F32 = jnp.float32


def _dot(a, b):
    return jnp.dot(a, b, preferred_element_type=_F32)


def _dot_nt(a, b):
    return lax.dot_general(a, b, (((1,), (1,)), ((), ())), preferred_element_type=_F32)


def _rms_norm(x, g):
    ms = jnp.mean(x * x, axis=-1, keepdims=True)
    return x * lax.rsqrt(ms + RMS_EPS) * g


def _silu_of_half(xh):
    return xh + xh * jnp.tanh(xh)


def _const_spec(shape):
    nd = len(shape)
    return pl.BlockSpec(shape, lambda *_: (0,) * nd, pipeline_mode=pl.Buffered(1))


def _w_in_spec(col, width):
    assert col % width == 0
    return pl.BlockSpec((D_MODEL, width), lambda *_: (0, col // width), pipeline_mode=pl.Buffered(1))


def _qkv_kernel(x_ref, g_ref, wq_ref, wk_ref, wv_ref, o_ref, h_ref):
    h = _rms_norm(x_ref[0], g_ref[...]).astype(_BF16)
    h_ref[0] = h
    for i, w_ref in enumerate((wq_ref, wk_ref, wv_ref)):
        o_ref[0, :, i * ATTN_WIDTH:(i + 1) * ATTN_WIDTH] = _dot(h, w_ref[...]).astype(_BF16)


def _qkv_call(x, g, w_l):
    B, T, D = x.shape
    tm = QKV_TILE
    return pl.pallas_call(
        _qkv_kernel,
        grid=(B, T // tm),
        in_specs=[
            pl.BlockSpec((1, tm, D), lambda b, t: (b, t, 0)),
            _const_spec((1, D)),
            _w_in_spec(COL_QKV, ATTN_WIDTH),
            _w_in_spec(COL_QKV + ATTN_WIDTH, ATTN_WIDTH),
            _w_in_spec(COL_QKV + 2 * ATTN_WIDTH, ATTN_WIDTH),
        ],
        out_specs=[
            pl.BlockSpec((1, tm, 3 * ATTN_WIDTH), lambda b, t: (b, t, 0)),
            pl.BlockSpec((1, tm, D), lambda b, t: (b, t, 0)),
        ],
        out_shape=[
            jax.ShapeDtypeStruct((B, T, 3 * ATTN_WIDTH), _BF16),
            jax.ShapeDtypeStruct((B, T, D), _BF16),
        ],
        compiler_params=pltpu.CompilerParams(
            dimension_semantics=("arbitrary", "arbitrary"), vmem_limit_bytes=VMEM_LIMIT),
        name="qkv_proj",
    )(x, g, w_l, w_l, w_l)


def _softplus(z):
    return jnp.maximum(z, 0.0) + jnp.log(1.0 + jnp.exp2(jnp.abs(z) * (-LOG2E)))


def _split_bf16(a):
    hi = a.astype(_BF16)
    lo = (a - hi.astype(_F32)).astype(_BF16)
    return hi, lo


def _attn_kernel(q_ref, k_ref, v_ref, ss_ref, o_ref, tot_ref):
    step = pl.program_id(1)
    sub = ATTN_SUB
    nsub = ATTN_TILE // sub
    far = ATTN_FAR_ROWS
    scale = 1.0 / float(np.sqrt(HEAD_DIM))
    zero = jnp.zeros((), _BF16)
    ss = ss_ref[...]

    def head_halves(a):
        low_half = lax.broadcasted_iota(jnp.int32, a.shape, 1) < HEAD_DIM
        return jnp.concatenate([jnp.where(low_half, a, zero), jnp.where(low_half, zero, a)], axis=0)

    def suffix_and_total(sp_blk):
        hi, lo = _split_bf16(sp_blk)
        st = _dot(jnp.concatenate([hi, lo], axis=1), ss)
        return st[:, :sub], st[:, sub:]

    def window(sb, _):
        g = step * nsub + sb
        rows = pl.ds(pl.multiple_of(sb * sub, sub), sub)
        far_rows = pl.ds(pl.multiple_of(sb * sub, sub), far)
        starts = [pl.multiple_of(jnp.maximum(g - 2 + j, 0) * sub, sub) for j in range(ATTN_WINDOW)]

        def bias_for(n_rows, first_block, n_blocks):
            key_pos = (g - 2 + first_block) * sub + lax.broadcasted_iota(jnp.int32, (n_rows, n_blocks * sub), 1)
            q_pos = g * sub + lax.broadcasted_iota(jnp.int32, (n_rows, n_blocks * sub), 0)
            return jnp.where(jnp.logical_and(key_pos >= 0, key_pos < q_pos), 0.0, MASK_BIAS)

        bias_far = bias_for(far, 0, 1)
        bias_near = bias_for(sub, 1, 2)
        for pair in range(N_HEADS // 2):
            lanes = slice(pair * LANES, (pair + 1) * LANES)
            k_blk = [k_ref[0, pl.ds(st, sub), lanes] for st in starts]
            v_win = jnp.concatenate([v_ref[0, pl.ds(st, sub), lanes] for st in starts], axis=0)
            z_near2 = _dot_nt(head_halves(q_ref[0, rows, lanes] * scale), jnp.concatenate(k_blk[1:], axis=0))
            z_far2 = _dot_nt(head_halves(q_ref[0, far_rows, lanes] * scale), k_blk[0])
            ws = []
            for hh in range(2):
                z = z_near2[hh * sub:(hh + 1) * sub] + bias_near
                sp = _softplus(z)
                sfx, later = suffix_and_total(sp[:, sub:])
                w_diag = jnp.exp(z[:, sub:] - sfx)
                sfx, tot = suffix_and_total(sp[:, :sub])
                w_prev = jnp.exp(z[:, :sub] - sfx - later)
                later = later + tot
                zf = z_far2[hh * far:(hh + 1) * far] + bias_far
                sfx, tot = suffix_and_total(_softplus(zf))
                w_far = jnp.exp(zf - sfx - later[:far])
                tot_ref[sb * N_HEADS + 2 * pair + hh] = jnp.concatenate([later[:far] + tot, later[far:]], axis=0)
                w_far = jnp.concatenate([w_far, jnp.zeros((sub - far, sub), _F32)], axis=0)
                ws.append(jnp.concatenate([w_far, w_prev, w_diag], axis=1).astype(_BF16))
            o_ref[0, rows, lanes] = _dot(jnp.concatenate(ws, axis=1), head_halves(v_win))
        return 0

    lax.fori_loop(0, nsub, window, 0, unroll=True)

    def least_total(lo, n):
        least = tot_ref[lo]
        for i in range(1, n):
            least = jnp.minimum(least, tot_ref[lo + i])
        return jnp.min(least)

    def tail(sb, _):
        g = step * nsub + sb
        rows = pl.ds(pl.multiple_of(sb * sub, sub), sub)

        def more(state):
            kb, least = state
            return jnp.logical_and(kb >= 0, least < EXP_ZERO_F32)

        def visit(state):
            kb, _ = state
            keys = pl.ds(pl.multiple_of(kb * sub, sub), sub)
            done = jnp.logical_and(kb == g - 2, lax.broadcasted_iota(jnp.int32, (sub, sub), 0) < far)
            skip_bias = jnp.where(done, MASK_BIAS, 0.0)
            for pair in range(N_HEADS // 2):
                lanes = slice(pair * LANES, (pair + 1) * LANES)
                z2 = _dot_nt(head_halves(q_ref[0, rows, lanes] * scale), k_ref[0, keys, lanes])
                ws = []
                for hh in range(2):
                    h = sb * N_HEADS + 2 * pair + hh
                    z = z2[hh * sub:(hh + 1) * sub] + skip_bias
                    sfx, tot = suffix_and_total(_softplus(z))
                    ws.append(jnp.exp(z - sfx - tot_ref[h]).astype(_BF16))
                    tot_ref[h] += tot
                o_ref[0, rows, lanes] += _dot(jnp.concatenate(ws, axis=1), head_halves(v_ref[0, keys, lanes]))
            return kb - 1, least_total(sb * N_HEADS, N_HEADS)

        lax.while_loop(more, visit, (g - 2, least_total(sb * N_HEADS, N_HEADS)))
        return 0

    @pl.when(least_total(0, nsub * N_HEADS) < EXP_ZERO_F32)
    def _():
        lax.fori_loop(0, nsub, tail, 0)


def _attn_call(qkv, ss):
    B, T, _ = qkv.shape
    return pl.pallas_call(
        _attn_kernel,
        grid=(B, T // ATTN_TILE),
        in_specs=[
            pl.BlockSpec((1, ATTN_TILE, ATTN_WIDTH), lambda b, i: (b, i, 0)),
            pl.BlockSpec((1, T, ATTN_WIDTH), lambda b, i: (b, 0, 1)),
            pl.BlockSpec((1, T, ATTN_WIDTH), lambda b, i: (b, 0, 2)),
            _const_spec(ss.shape),
        ],
        out_specs=pl.BlockSpec((1, ATTN_TILE, ATTN_WIDTH), lambda b, i: (b, i, 0)),
        out_shape=jax.ShapeDtypeStruct((B, T, ATTN_WIDTH), _F32),
        scratch_shapes=[
            pltpu.VMEM((ATTN_TILE // ATTN_SUB * N_HEADS, ATTN_SUB, ATTN_SUB), _F32),
        ],
        compiler_params=pltpu.CompilerParams(
            dimension_semantics=("arbitrary", "arbitrary"), vmem_limit_bytes=VMEM_LIMIT),
        name="stickbreak_attn",
    )(qkv, qkv, qkv, ss)


def _suffix_matrix():
    j = np.arange(ATTN_SUB)[:, None]
    s = np.arange(ATTN_SUB)[None, :]
    from_s_on = (j >= s).astype(np.float32)
    half = np.concatenate([from_s_on, np.ones_like(from_s_on)], axis=1)
    return jnp.asarray(np.concatenate([half, half], axis=0), dtype=_BF16)


def _block_kernel(x_ref, h_ref, o_ref, wp_ref, wgp_ref, wc2_ref, wgc_ref, wga_ref, wgm0_ref, wgm1_ref,
                  poolw_ref, poolb_ref, pools_ref, wpo_ref, convw_ref, convb_ref, lng_ref, lnb_ref,
                  wco_ref, wao_ref, wo_ref, npost_ref, out_ref, pbuf, ubuf, cbuf):
    tm = TOKEN_TILE
    ti = pl.program_id(1)

    @pl.when(ti == 0)
    def _():
        pbuf[:, 0:POOL_HALO, :] = jnp.zeros((LANE_TILES, POOL_HALO, LANES), _F32)
        ubuf[:, 0:CONV_HALO, :] = jnp.zeros((LANE_TILES, CONV_HALO, LANES), _F32)

    x = x_ref[0]
    h = h_ref[0]

    p = _dot(h, wp_ref[...])
    t_glob = ti * tm + lax.broadcasted_iota(jnp.int32, (tm, POOL_GROUP_DIM), 0)
    ys = []
    for g, win in enumerate(POOL_WINDOWS):
        pg = p[:, g * LANES:(g + 1) * LANES]
        pbuf[g, POOL_HALO:POOL_HALO + tm, :] = pg
        s = pg
        for i in range(1, win):
            s = s + pbuf[g, POOL_HALO - i:POOL_HALO - i + tm, :]
        pbuf[g, 0:POOL_HALO, :] = pg[tm - POOL_HALO:, :]
        count = jnp.minimum(t_glob + 1, win).astype(_F32)
        dg = s / count - pg
        ys.append(_dot(dg.astype(_BF16), poolw_ref[g]))
    y = (jnp.concatenate(ys, axis=-1) + poolb_ref[...]) * pools_ref[...]
    ua = y * _silu_of_half(_dot(h, wgp_ref[...]))
    ya = _dot(ua.astype(_BF16), wpo_ref[...])

    c2 = _dot(h, wc2_ref[...])
    a_half = c2[:, :CONV_WIDTH]
    u = a_half + a_half * jnp.tanh(c2[:, CONV_WIDTH:])
    for j in range(LANE_TILES):
        ubuf[j, CONV_HALO:CONV_HALO + tm, :] = u[:, j * LANES:(j + 1) * LANES]
    for ci in range(tm // CONV_ROWS):
        r0 = ci * CONV_ROWS
        for j in range(LANE_TILES):
            lanes = slice(j * LANES, (j + 1) * LANES)
            acc = jnp.broadcast_to(convb_ref[:, lanes], (CONV_ROWS, LANES))
            for k in range(CONV_KERNEL):
                off = r0 + CONV_HALO - (CONV_KERNEL - 1) + k
                acc = acc + convw_ref[k:k + 1, lanes] * ubuf[j, off:off + CONV_ROWS, :]
            cbuf[r0:r0 + CONV_ROWS, lanes] = acc
    for j in range(LANE_TILES):
        ubuf[j, 0:CONV_HALO, :] = u[tm - CONV_HALO:, j * LANES:(j + 1) * LANES]
    cv = cbuf[...]
    mu = jnp.mean(cv, axis=-1, keepdims=True)
    var = jnp.mean(jnp.square(cv - mu), axis=-1, keepdims=True)
    ln_half = (cv - mu) * lax.rsqrt(var + LN_EPS) * lng_ref[...] + lnb_ref[...]
    ub = _silu_of_half(ln_half) * _silu_of_half(_dot(h, wgc_ref[...]))
    yb = _dot(ub.astype(_BF16), wco_ref[...])

    uc = o_ref[0] * _silu_of_half(_dot(h, wga_ref[...]))
    yc = _dot(uc.astype(_BF16), wao_ref[...])

    t = jnp.tanh(jnp.concatenate([_dot(h, wgm0_ref[...]), _dot(h, wgm1_ref[...])], axis=-1))
    m2 = (t[:, :D_MODEL] * ya + t[:, D_MODEL:2 * D_MODEL] * yb + t[:, 2 * D_MODEL:] * yc) + (ya + yb + yc)
    out = _dot(m2.astype(_BF16), wo_ref[...])
    out_ref[0] = x + _rms_norm(out, npost_ref[...])


def _block_call(x, h, o, w_l, poolw, poolb, pools, wpo, convw, convb, lng, lnb, wco, wao, wo, npost):
    B, T, D = x.shape
    tm = TOKEN_TILE
    gm_half = (COL_END - COL_GM) // 2
    tail_consts = (poolw, poolb, pools, wpo, convw, convb, lng, lnb, wco, wao, wo, npost)
    return pl.pallas_call(
        _block_kernel,
        grid=(B, T // tm),
        in_specs=[
            pl.BlockSpec((1, tm, D), lambda b, t: (b, t, 0)),
            pl.BlockSpec((1, tm, D), lambda b, t: (b, t, 0)),
            pl.BlockSpec((1, tm, ATTN_WIDTH), lambda b, t: (b, t, 0)),
            _w_in_spec(COL_P, POOL_WIDTH),
            _w_in_spec(COL_GP, POOL_WIDTH),
            _w_in_spec(COL_C2, 2 * CONV_WIDTH),
            _w_in_spec(COL_GC, CONV_WIDTH),
            _w_in_spec(COL_GA, ATTN_WIDTH),
            _w_in_spec(COL_GM, gm_half),
            _w_in_spec(COL_GM + gm_half, gm_half),
        ] + [_const_spec(c.shape) for c in tail_consts],
        out_specs=pl.BlockSpec((1, tm, D), lambda b, t: (b, t, 0)),
        out_shape=jax.ShapeDtypeStruct((B, T, D), _F32),
        scratch_shapes=[
            pltpu.VMEM((LANE_TILES, POOL_HALO + tm, LANES), _F32),
            pltpu.VMEM((LANE_TILES, CONV_HALO + tm, LANES), _F32),
            pltpu.VMEM((tm, CONV_WIDTH), _F32),
        ],
        compiler_params=pltpu.CompilerParams(
            dimension_semantics=("arbitrary", "arbitrary"), vmem_limit_bytes=VMEM_LIMIT),
        name="mixer_block",
    )(x, h, o, w_l, w_l, w_l, w_l, w_l, w_l, w_l, *tail_consts)


def _half_gate_scale():
    scale = np.full((1, COL_END), 0.5, np.float32)
    scale[:, COL_P:COL_GP] = 1.0
    scale[:, COL_QKV:COL_GA] = 1.0
    return jnp.asarray(scale)


def kernel(x, norm_pre, w_in, pool_w, pool_b, pool_scale, w_pool_out, conv_w, conv_b, conv_ln_g, conv_ln_b,
           w_conv_out, w_attn_out, w_o, norm_post):
    depth = w_in.shape[0]
    ss = _suffix_matrix()
    gate_scale = _half_gate_scale()
    row = lambda a: a.reshape(1, -1)
    for l in range(depth):
        w_l = (w_in[l] * gate_scale).astype(_BF16)
        qkv, h = _qkv_call(x, row(norm_pre[l]), w_l)
        o = _attn_call(qkv, ss)
        x = _block_call(
            x, h, o, w_l, pool_w[l].astype(_BF16), row(pool_b[l]), row(pool_scale[l]),
            w_pool_out[l].astype(_BF16), conv_w[l], row(conv_b[l]), row(0.5 * conv_ln_g[l]),
            row(0.5 * conv_ln_b[l]), w_conv_out[l].astype(_BF16), w_attn_out[l].astype(_BF16),
            (0.5 * w_o[l]).astype(_BF16), row(norm_post[l]))
    return x
```
